```python
import jax, jax.numpy as jnp
from jax import lax
import numpy as np

D_MODEL = 1024
BATCH = 16
SEQ = 2048
DEPTH = 1

D_POOL = D_MODEL
POOL_GROUPS = 4
POOL_GROUP_DIM = D_POOL // POOL_GROUPS
POOL_WINDOWS = (2, 4, 8, 16)
D_MLSTM = D_MODEL
MLSTM_HEADS = 4
HEAD_DIM = D_MLSTM // MLSTM_HEADS
N_DIRS = 2
CONV_WIDTH = 5
CHUNK = 128
N_BRANCHES = 2
EPS = 1e-6
NEG_INIT = -1e30
SPLIT_SIZES = (D_POOL, D_POOL, D_MLSTM, D_MLSTM, D_MLSTM, D_MLSTM, D_MLSTM,
               2 * N_DIRS * MLSTM_HEADS, D_MODEL, D_MODEL)
D_IN = 2 * D_POOL + 5 * D_MLSTM + 2 * N_DIRS * MLSTM_HEADS + N_BRANCHES * D_MODEL

kernel_name = "hybrid_pool_mlstm_gated_block"


def rms_norm(x, g):
    xf = x.astype(jnp.float32)
    y = xf * lax.rsqrt(jnp.mean(xf * xf, axis=-1, keepdims=True) + EPS)
    return (y * g.astype(jnp.float32)).astype(x.dtype)


def multiscale_pool(u):
    B, S, _ = u.shape
    uf = u.astype(jnp.float32)
    cs = jnp.concatenate([jnp.zeros((B, 1, D_POOL), jnp.float32), jnp.cumsum(uf, axis=1)], axis=1)
    t = jnp.arange(S)
    outs = []
    for g, w in enumerate(POOL_WINDOWS):
        lo = jnp.clip(t - w // 2, 0, S - 1)
        hi = jnp.clip(t + (w - 1 - w // 2), 0, S - 1)
        sl = slice(g * POOL_GROUP_DIM, (g + 1) * POOL_GROUP_DIM)
        csg = cs[:, :, sl]
        window_sum = jnp.take(csg, hi + 1, axis=1) - jnp.take(csg, lo, axis=1)
        count = (hi - lo + 1).astype(jnp.float32)[None, :, None]
        outs.append(window_sum / count - uf[:, :, sl])
    return jnp.concatenate(outs, axis=-1)


def centred_depthwise_conv(x, w, b):
    C = x.shape[-1]
    y = lax.conv_general_dilated(
        x, w[:, None, :].astype(x.dtype), window_strides=(1,),
        padding=[(CONV_WIDTH // 2, CONV_WIDTH // 2)],
        dimension_numbers=('NWC', 'WIO', 'NWC'), feature_group_count=C)
    return y + b.astype(x.dtype)


def mlstm_direction(q, k, v, i_pre, f_pre):
    B, H, S, Dh = q.shape
    nc = S // CHUNK
    qc = q.reshape(B, H, nc, CHUNK, Dh) * (Dh ** -0.5)
    kc = k.reshape(B, H, nc, CHUNK, Dh)
    vc = v.reshape(B, H, nc, CHUNK, Dh)
    log_f = jax.nn.log_sigmoid(f_pre).reshape(B, H, nc, CHUNK)
    log_i = i_pre.reshape(B, H, nc, CHUNK)
    b = jnp.cumsum(log_f, axis=-1)
    b_last = b[..., -1]

    w_end = b_last[..., None] - b + log_i
    m_chunk = jnp.max(w_end, axis=-1)
    kw = kc * jnp.exp(w_end - m_chunk[..., None])[..., None]
    c_chunk = jnp.einsum('bhcld,bhcle->bhcde', kw, vc)
    n_chunk = jnp.sum(kw, axis=3)

    def step(carry, xs):
        c_st, n_st, m_st = carry
        c_add, n_add, m_add, bl, q_blk = xs
        num_inter = jnp.einsum('bhld,bhde->bhle', q_blk, c_st)
        den_inter = jnp.einsum('bhld,bhd->bhl', q_blk, n_st)
        m_new = jnp.maximum(bl + m_st, m_add)
        keep = jnp.exp(bl + m_st - m_new)
        add = jnp.exp(m_add - m_new)
        c_new = keep[..., None, None] * c_st + add[..., None, None] * c_add
        n_new = keep[..., None] * n_st + add[..., None] * n_add
        return (c_new, n_new, m_new), (num_inter, den_inter, m_st)

    init = (jnp.zeros((B, H, Dh, Dh), jnp.float32),
            jnp.zeros((B, H, Dh), jnp.float32),
            jnp.full((B, H), NEG_INIT, jnp.float32))
    xs = (jnp.moveaxis(c_chunk, 2, 0), jnp.moveaxis(n_chunk, 2, 0), jnp.moveaxis(m_chunk, 2, 0),
          jnp.moveaxis(b_last, 2, 0), jnp.moveaxis(qc, 2, 0))
    _, (num_inter, den_inter, m_prev) = lax.scan(step, init, xs)
    num_inter = jnp.moveaxis(num_inter, 0, 2)
    den_inter = jnp.moveaxis(den_inter, 0, 2)
    m_prev = jnp.moveaxis(m_prev, 0, 2)

    mask = jnp.tril(jnp.ones((CHUNK, CHUNK), dtype=bool))
    log_d = jnp.where(mask, b[..., :, None] - b[..., None, :] + log_i[..., None, :], -jnp.inf)
    log_inter = b + m_prev[..., None]
    m_t = jnp.maximum(jnp.max(log_d, axis=-1), log_inter)
    d = jnp.exp(log_d - m_t[..., None])
    inter_scale = jnp.exp(log_inter - m_t)
    scores = jnp.einsum('bhcld,bhcsd->bhcls', qc, kc) * d
    num = jnp.einsum('bhcls,bhcsd->bhcld', scores, vc) + inter_scale[..., None] * num_inter
    den = jnp.sum(scores, axis=-1) + inter_scale * den_inter
    h = num / jnp.maximum(jnp.abs(den), jnp.exp(-m_t))[..., None]
    return h.reshape(B, H, S, Dh)


def hybrid_layer(x, norm_pre_g, w_in, conv_qk_w, conv_qk_b, b_igate, b_fgate, pool_mix_w,
                 pool_scale, mlstm_norm_g, w_proj_pool, w_proj_mlstm, b_merge, w_out, norm_post_g):
    B, S, _ = x.shape
    h = rms_norm(x, norm_pre_g)
    proj = jnp.einsum('bsd,de->bse', h, w_in.astype(h.dtype))
    split_points = [int(p) for p in np.cumsum(SPLIT_SIZES)[:-1]]
    u_a, z_a, q, k, v, o, z_b, gates, g_a, g_b = jnp.split(proj, split_points, axis=-1)

    pooled = multiscale_pool(u_a)
    mixed = jnp.einsum('bsgc,gcd->bsgd', pooled.reshape(B, S, POOL_GROUPS, POOL_GROUP_DIM),
                       pool_mix_w.astype(jnp.float32)).reshape(B, S, D_POOL)
    y_a = (mixed * pool_scale.astype(jnp.float32)).astype(x.dtype) * jax.nn.silu(z_a)

    qk = jax.nn.silu(centred_depthwise_conv(jnp.concatenate([q, k], axis=-1), conv_qk_w, conv_qk_b))
    q, k = jnp.split(qk, 2, axis=-1)

    def to_heads(t):
        return t.astype(jnp.float32).reshape(B, S, MLSTM_HEADS, HEAD_DIM).transpose(0, 2, 1, 3)

    qh, kh, vh = to_heads(q), to_heads(k), to_heads(v)
    gates = gates.astype(jnp.float32).reshape(B, S, N_DIRS, 2, MLSTM_HEADS)
    i_pre = (gates[:, :, :, 0, :] + b_igate.astype(jnp.float32)).transpose(2, 0, 3, 1)
    f_pre = (gates[:, :, :, 1, :] + b_fgate.astype(jnp.float32)).transpose(2, 0, 3, 1)
    h_fwd = mlstm_direction(qh, kh, vh, i_pre[0], f_pre[0])
    flip = lambda t: jnp.flip(t, axis=2)
    h_bwd = flip(mlstm_direction(flip(qh), flip(kh), flip(vh), flip(i_pre[1]), flip(f_pre[1])))
    hb = h_fwd + h_bwd
    hb = hb * lax.rsqrt(jnp.mean(hb * hb, axis=-1, keepdims=True) + EPS)
    hb = hb.transpose(0, 2, 1, 3).reshape(B, S, D_MLSTM) * mlstm_norm_g.astype(jnp.float32)
    y_b = hb.astype(x.dtype) * jax.nn.sigmoid(o) * jax.nn.silu(z_b)

    p_a = jnp.einsum('bsc,cd->bsd', y_a, w_proj_pool.astype(y_a.dtype))
    p_b = jnp.einsum('bsc,cd->bsd', y_b, w_proj_mlstm.astype(y_b.dtype))
    merged = jax.nn.sigmoid(g_a + b_merge[0]) * p_a + jax.nn.sigmoid(g_b + b_merge[1]) * p_b
    out = jnp.einsum('bsd,de->bse', merged, w_out.astype(merged.dtype))
    return x + rms_norm(out, norm_post_g)


def setup_inputs(seed: int = 0) -> dict:
    key = jax.random.key(seed)
    ks = jax.random.split(key, 16)
    n = jax.random.normal
    f32 = jnp.float32
    x = n(ks[0], (BATCH, SEQ, D_MODEL), f32)
    norm_pre_g = 1.0 + 0.1 * n(ks[1], (DEPTH, D_MODEL), f32)
    w_in = n(ks[2], (DEPTH, D_MODEL, D_IN), f32) * D_MODEL ** -0.5
    conv_qk_w = n(ks[3], (DEPTH, CONV_WIDTH, 2 * D_MLSTM), f32) * CONV_WIDTH ** -0.5
    conv_qk_b = 0.02 * n(ks[4], (DEPTH, 2 * D_MLSTM), f32)
    b_igate = 0.1 * n(ks[5], (DEPTH, N_DIRS, MLSTM_HEADS), f32)
    b_fgate = jnp.linspace(3.0, 6.0, MLSTM_HEADS, dtype=f32)[None, None, :] + 0.1 * n(ks[6], (DEPTH, N_DIRS, MLSTM_HEADS), f32)
    pool_mix_w = n(ks[7], (DEPTH, POOL_GROUPS, POOL_GROUP_DIM, POOL_GROUP_DIM), f32) * POOL_GROUP_DIM ** -0.5
    pool_scale = 1.0 + 0.1 * n(ks[8], (DEPTH, D_POOL), f32)
    mlstm_norm_g = 1.0 + 0.1 * n(ks[9], (DEPTH, D_MLSTM), f32)
    w_proj_pool = n(ks[10], (DEPTH, D_POOL, D_MODEL), f32) * D_POOL ** -0.5
    w_proj_mlstm = n(ks[11], (DEPTH, D_MLSTM, D_MODEL), f32) * D_MLSTM ** -0.5
    b_merge = 0.1 * n(ks[12], (DEPTH, N_BRANCHES, D_MODEL), f32)
    w_out = n(ks[13], (DEPTH, D_MODEL, D_MODEL), f32) * D_MODEL ** -0.5
    norm_post_g = 1.0 + 0.1 * n(ks[14], (DEPTH, D_MODEL), f32)
    return {"x": x, "norm_pre_g": norm_pre_g, "w_in": w_in, "conv_qk_w": conv_qk_w,
            "conv_qk_b": conv_qk_b, "b_igate": b_igate, "b_fgate": b_fgate,
            "pool_mix_w": pool_mix_w, "pool_scale": pool_scale, "mlstm_norm_g": mlstm_norm_g,
            "w_proj_pool": w_proj_pool, "w_proj_mlstm": w_proj_mlstm, "b_merge": b_merge,
            "w_out": w_out, "norm_post_g": norm_post_g}


def reference(x, norm_pre_g, w_in, conv_qk_w, conv_qk_b, b_igate, b_fgate, pool_mix_w, pool_scale,
              mlstm_norm_g, w_proj_pool, w_proj_mlstm, b_merge, w_out, norm_post_g):
    for l in range(DEPTH):
        x = hybrid_layer(x, norm_pre_g[l], w_in[l], conv_qk_w[l], conv_qk_b[l], b_igate[l], b_fgate[l],
                         pool_mix_w[l], pool_scale[l], mlstm_norm_g[l], w_proj_pool[l], w_proj_mlstm[l],
                         b_merge[l], w_out[l], norm_post_g[l])
    return x
```

```python
import functools

import jax
import jax.numpy as jnp
from jax import lax
from jax.experimental import pallas as pl
from jax.experimental.pallas import tpu as pltpu

F32 = jnp.float32
BF16 = jnp.bfloat16

HEADS = 4
N_DIRS = 2
POOL_WINDOWS = (2, 4, 8, 16)
CONV_WIDTH = 5
CHUNK = 128
EPS = 1e-6
NEG_INIT = -1e30
HALO = 8
V7X_VMEM_LIMIT_BYTES = 56 * 1024 * 1024

G_A, G_EP, G_KEEP, G_MPREV, G_M, G_NM = 0, 2, 4, 6, 8, 10
G_ROWS = 16


def _sigmoid(x):
    return 1.0 / (1.0 + jnp.exp(-x))


def _silu(x):
    return x * _sigmoid(x)


def _log_sigmoid(x):
    return jnp.minimum(x, 0.0) - jnp.log1p(jnp.exp(-jnp.abs(x)))


def _rms_rows(x, g):
    ms = jnp.mean(x * x, axis=-1, keepdims=True)
    return x * lax.rsqrt(ms + EPS) * g


def _gate_tables(gt, gbias, gate_scr, *, nc):
    L = CHUNK
    lane = lax.broadcasted_iota(jnp.int32, (8, L), 1)
    row = lax.broadcasted_iota(jnp.int32, (8, L), 0)
    fwd = row < HEADS
    bi = gbias[0:8, :]
    bf = gbias[8:16, :]

    a_l, b_l, bl_l, mc_l, pm_l = [], [], [], [], []
    for c in range(nc):
        log_i = gt[0:8, c * L:(c + 1) * L] + bi
        log_f = _log_sigmoid(gt[8:16, c * L:(c + 1) * L] + bf)
        pre, suf = log_f, log_f
        sh = 1
        while sh < L:
            pre = pre + jnp.where(lane >= sh, pltpu.roll(pre, sh, 1), 0.0)
            suf = suf + jnp.where(lane < L - sh, pltpu.roll(suf, L - sh, 1), 0.0)
            sh *= 2
        b = jnp.where(fwd, pre, suf)
        bl = jnp.sum(log_f, axis=1, keepdims=True)
        a = log_i - b
        pmx, smx = a, a
        sh = 1
        while sh < L:
            pmx = jnp.maximum(pmx, jnp.where(lane >= sh, pltpu.roll(pmx, sh, 1), -jnp.inf))
            smx = jnp.maximum(smx, jnp.where(lane < L - sh, pltpu.roll(smx, L - sh, 1), -jnp.inf))
            sh *= 2
        a_l.append(a)
        b_l.append(b)
        bl_l.append(bl)
        mc_l.append(bl + jnp.max(a, axis=1, keepdims=True))
        pm_l.append(jnp.where(fwd, pmx, smx))

    fwd_col = fwd[:, 0:1]
    mprev = [None] * nc
    mnew = [None] * nc
    m_f = jnp.full((8, 1), NEG_INIT, F32)
    m_b = jnp.full((8, 1), NEG_INIT, F32)
    mprev_f, mnew_f, mprev_b, mnew_b = {}, {}, {}, {}
    for j in range(nc):
        cf, cb = j, nc - 1 - j
        mprev_f[cf] = m_f
        m_f = jnp.maximum(bl_l[cf] + m_f, mc_l[cf])
        mnew_f[cf] = m_f
        mprev_b[cb] = m_b
        m_b = jnp.maximum(bl_l[cb] + m_b, mc_l[cb])
        mnew_b[cb] = m_b
    for c in range(nc):
        mprev[c] = jnp.where(fwd_col, mprev_f[c], mprev_b[c])
        mnew[c] = jnp.where(fwd_col, mnew_f[c], mnew_b[c])

    for c in range(nc):
        a, b, bl = a_l[c], b_l[c], bl_l[c]
        mp = jnp.broadcast_to(mprev[c], (8, L))
        ep = jnp.exp(bl + a - mnew[c])
        keep = jnp.broadcast_to(jnp.exp(bl + mprev[c] - mnew[c]), (8, L))
        m_row = jnp.maximum(pm_l[c], mp)
        nm = jnp.exp(-(b + m_row))
        for hd in range(HEADS):
            for d in range(N_DIRS):
                r = d * HEADS + hd
                for base, val in ((G_A, a), (G_EP, ep), (G_KEEP, keep), (G_MPREV, mp),
                                  (G_M, m_row), (G_NM, nm)):
                    gate_scr[hd, c, base + d:base + d + 1, :] = val[r:r + 1, :]


def _mlstm_kernel(x_ref, gpre_ref, wgt_ref, gbias_ref, wm_ref, cw_ref, ng_ref,
                  yb_ref,
                  hn_scr, gate_scr, conv_scr, q_scr, k_scr, kt_scr, v_scr, og_scr,
                  d_scr, cs_scr, ns_scr, nch_scr, c_scr,
                  *, S, D, Dh, NC):
    L = CHUNK
    h = pl.program_id(1)

    @pl.when(h == 0)
    def _prep():
        gpre = gpre_ref[...]
        for blk in range(S // L):
            xb = x_ref[0, blk * L:(blk + 1) * L, :]
            hn_scr[blk * L:(blk + 1) * L, :] = _rms_rows(xb, gpre).astype(BF16)
        gt = lax.dot_general(wgt_ref[...], hn_scr[...], (((1,), (1,)), ((), ())),
                             preferred_element_type=F32)
        _gate_tables(gt, gbias_ref[...], gate_scr, nc=NC)
        zero_halo = jnp.zeros((HALO, Dh), F32)
        for t in range(2):
            conv_scr[t, 0:HALO, :] = zero_halo
            conv_scr[t, HALO + S:HALO + S + HALO, :] = zero_halo

    hn = hn_scr[...]
    conv_scr[0, HALO:HALO + S, :] = jnp.dot(hn, wm_ref[0, :, 0:Dh], preferred_element_type=F32)
    conv_scr[1, HALO:HALO + S, :] = jnp.dot(hn, wm_ref[0, :, Dh:2 * Dh], preferred_element_type=F32)
    v_all = jnp.dot(hn, wm_ref[0, :, 2 * Dh:3 * Dh], preferred_element_type=F32)
    o_all = jnp.dot(hn, wm_ref[0, :, 3 * Dh:4 * Dh], preferred_element_type=F32)
    z_all = jnp.dot(hn, wm_ref[0, :, 4 * Dh:5 * Dh], preferred_element_type=F32)
    og_all = _sigmoid(o_all) * _silu(z_all)
    for c in range(NC):
        v_scr[c] = v_all[c * L:(c + 1) * L, :].astype(BF16)
        og_scr[c] = og_all[c * L:(c + 1) * L, :]

    qscale = Dh ** -0.5
    for c in range(NC):
        for t in range(2):
            acc = jnp.broadcast_to(cw_ref[0, CONV_WIDTH:CONV_WIDTH + 1, t * Dh:(t + 1) * Dh], (L, Dh))
            for j in range(CONV_WIDTH):
                off = HALO + c * L + j - CONV_WIDTH // 2
                acc = acc + conv_scr[t, off:off + L, :] * cw_ref[0, j:j + 1, t * Dh:(t + 1) * Dh]
            act = _silu(acc)
            if t == 0:
                q_scr[c] = (act * qscale).astype(BF16)
            else:
                k_scr[c] = act.astype(BF16)
                kt_scr[c] = act.T.astype(BF16)

    for d in range(N_DIRS):
        for c in range(NC):
            ep = gate_scr[h, c, G_EP + d:G_EP + d + 1, :]
            kw_t = kt_scr[c] * ep.astype(BF16)
            d_scr[c] = jnp.dot(kw_t, v_scr[c], preferred_element_type=F32)
            ep8 = jnp.broadcast_to(ep, (8, L)).astype(BF16)
            nch_scr[c] = jnp.dot(ep8, k_scr[c], preferred_element_type=F32)
        c_scr[...] = jnp.zeros((Dh, Dh), F32)
        n_st = jnp.zeros((8, Dh), F32)
        order = range(NC) if d == 0 else range(NC - 1, -1, -1)
        for c in order:
            cs_scr[d, c] = c_scr[...].astype(BF16)
            ns_scr[d, c] = n_st
            keep = gate_scr[h, c, G_KEEP + d:G_KEEP + d + 1, :]
            keep_d = jnp.concatenate([keep] * (Dh // L), axis=1)
            c_scr[...] = c_scr[...] * keep_d + d_scr[c]
            n_st = n_st * keep_d + nch_scr[c]

    li = lax.broadcasted_iota(jnp.int32, (L, L), 0)
    si = lax.broadcasted_iota(jnp.int32, (L, L), 1)
    eye = li == si
    masks = (si <= li, si >= li)
    ng = ng_ref[0]

    def chunk_body(c, carry):
        q_c = q_scr[c]
        v_c = v_scr[c]
        q_f = q_c.astype(F32)
        s_raw = jnp.dot(q_c, kt_scr[c], preferred_element_type=F32)
        hb = jnp.zeros((L, Dh), F32)
        for d in range(N_DIRS):
            g = gate_scr.at[h, c]
            a_b = jnp.broadcast_to(g[G_A + d:G_A + d + 1, :], (L, L))
            m_b = jnp.broadcast_to(g[G_M + d:G_M + d + 1, :], (L, L))
            nm_b = jnp.broadcast_to(g[G_NM + d:G_NM + d + 1, :], (L, L))
            mp = g[G_MPREV + d:G_MPREV + d + 1, 0:1]
            m_col = jnp.sum(jnp.where(eye, m_b, 0.0), axis=1, keepdims=True)
            nm_col = jnp.sum(jnp.where(eye, nm_b, 0.0), axis=1, keepdims=True)
            p = jnp.where(masks[d], jnp.exp(a_b - m_col), 0.0) * s_raw
            inter = jnp.exp(mp - m_col)
            n_row = ns_scr[d, c][0:1, :]
            den_inter = jnp.sum(q_f * n_row, axis=1, keepdims=True)
            den = jnp.sum(p, axis=1, keepdims=True) + inter * den_inter
            num = (jnp.dot(p.astype(BF16), v_c, preferred_element_type=F32)
                   + inter * jnp.dot(q_c, cs_scr[d, c], preferred_element_type=F32))
            hb = hb + num / jnp.maximum(jnp.abs(den), nm_col)
        ms = jnp.mean(hb * hb, axis=-1, keepdims=True)
        y = (hb * lax.rsqrt(ms + EPS) * ng) * og_scr[c]
        row0 = pl.multiple_of(c * L, L)
        yb_ref[0, pl.ds(row0, L), :] = y.astype(yb_ref.dtype)
        return carry

    lax.fori_loop(0, NC, chunk_body, 0, unroll=2)


def _mlstm_branch(x, gpre, wgt, gbias, wm, cw, ng):
    B, S, D = x.shape
    Dh = D // HEADS
    NC = S // CHUNK
    L = CHUNK
    kern = functools.partial(_mlstm_kernel, S=S, D=D, Dh=Dh, NC=NC)
    return pl.pallas_call(
        kern,
        grid=(B, HEADS),
        in_specs=[
            pl.BlockSpec((1, S, D), lambda b, h: (b, 0, 0)),
            pl.BlockSpec((1, D), lambda b, h: (0, 0)),
            pl.BlockSpec((16, D), lambda b, h: (0, 0)),
            pl.BlockSpec((16, L), lambda b, h: (0, 0)),
            pl.BlockSpec((1, D, 5 * Dh), lambda b, h: (h, 0, 0)),
            pl.BlockSpec((1, 8, 2 * Dh), lambda b, h: (h, 0, 0)),
            pl.BlockSpec((1, 1, Dh), lambda b, h: (h, 0, 0)),
        ],
        out_specs=pl.BlockSpec((1, S, Dh), lambda b, h: (b, 0, h)),
        out_shape=jax.ShapeDtypeStruct((B, S, D), BF16),
        scratch_shapes=[
            pltpu.VMEM((S, D), BF16),
            pltpu.VMEM((HEADS, NC, G_ROWS, L), F32),
            pltpu.VMEM((2, S + 2 * HALO, Dh), F32),
            pltpu.VMEM((NC, L, Dh), BF16),
            pltpu.VMEM((NC, L, Dh), BF16),
            pltpu.VMEM((NC, Dh, L), BF16),
            pltpu.VMEM((NC, L, Dh), BF16),
            pltpu.VMEM((NC, L, Dh), F32),
            pltpu.VMEM((NC, Dh, Dh), F32),
            pltpu.VMEM((N_DIRS, NC, Dh, Dh), BF16),
            pltpu.VMEM((N_DIRS, NC, 8, Dh), F32),
            pltpu.VMEM((NC, 8, Dh), F32),
            pltpu.VMEM((Dh, Dh), F32),
        ],
        compiler_params=pltpu.CompilerParams(
            dimension_semantics=("arbitrary", "arbitrary"),
            vmem_limit_bytes=V7X_VMEM_LIMIT_BYTES),
        name="mlstm_branch",
    )(x, gpre, wgt, gbias, wm, cw, ng)


def _token_kernel(xp_ref, x_ref, xn_ref, yb_ref, gpre_ref, wt_ref, mix_ref, pscale_ref,
                  wpp_ref, wpm_ref, bm_ref, wo_ref, gpost_ref,
                  out_ref,
                  u_scr,
                  *, S, D, TM):
    i = pl.program_id(1)
    nt = pl.num_programs(1)
    Dg = D // len(POOL_WINDOWS)
    x = x_ref[0]
    x_ext = jnp.concatenate([xp_ref[0], x, xn_ref[0]], axis=0)
    hn_ext = _rms_rows(x_ext, gpre_ref[...]).astype(BF16)

    u_ext = jnp.dot(hn_ext, wt_ref[:, 0:D], preferred_element_type=F32)
    has_prev = (i > 0).astype(F32)
    has_next = (i < nt - 1).astype(F32)
    u_scr[0:HALO, :] = u_ext[0:HALO, :] * has_prev
    u_scr[HALO:HALO + TM, :] = u_ext[HALO:HALO + TM, :]
    u_scr[HALO + TM:, :] = u_ext[HALO + TM:, :] * has_next

    z_a = jnp.dot(hn_ext, wt_ref[:, D:2 * D], preferred_element_type=F32)[HALO:HALO + TM, :]
    g_a = jnp.dot(hn_ext, wt_ref[:, 2 * D:3 * D], preferred_element_type=F32)[HALO:HALO + TM, :]
    g_b = jnp.dot(hn_ext, wt_ref[:, 3 * D:4 * D], preferred_element_type=F32)[HALO:HALO + TM, :]

    t_glob = i * TM + lax.broadcasted_iota(jnp.int32, (TM, Dg), 0)
    ya_parts = []
    for g, w in enumerate(POOL_WINDOWS):
        cols = slice(g * Dg, (g + 1) * Dg)
        before, after = w // 2, w - 1 - w // 2
        wsum = u_scr[HALO - before:HALO - before + TM, cols]
        for j in range(-before + 1, after + 1):
            wsum = wsum + u_scr[HALO + j:HALO + j + TM, cols]
        lo = jnp.maximum(t_glob - before, 0)
        hi = jnp.minimum(t_glob + after, S - 1)
        count = (hi - lo + 1).astype(F32)
        pooled = wsum / count - u_scr[HALO:HALO + TM, cols]
        mixed = jnp.dot(pooled.astype(BF16), mix_ref[g], preferred_element_type=F32)
        ya_parts.append((mixed * pscale_ref[:, cols]) * _silu(z_a[:, cols]))
    y_a = jnp.concatenate(ya_parts, axis=1).astype(BF16)

    p_a = jnp.dot(y_a, wpp_ref[...], preferred_element_type=F32)
    p_b = jnp.dot(yb_ref[0], wpm_ref[...], preferred_element_type=F32)
    merged = _sigmoid(g_a + bm_ref[0:1, :]) * p_a + _sigmoid(g_b + bm_ref[1:2, :]) * p_b
    out = jnp.dot(merged.astype(BF16), wo_ref[...], preferred_element_type=F32)
    out_ref[0] = x + _rms_rows(out, gpost_ref[...])


def _token_branch(x, yb, gpre, wt, mix, pscale, wpp, wpm, bm, wo, gpost, *, tm):
    B, S, D = x.shape
    nt = S // tm
    rb = tm // HALO
    n_halo_blocks = S // HALO
    kern = functools.partial(_token_kernel, S=S, D=D, TM=tm)
    const = lambda b, i: (0, 0)
    return pl.pallas_call(
        kern,
        grid=(B, nt),
        in_specs=[
            pl.BlockSpec((1, HALO, D), lambda b, i: (b, jnp.maximum(i * rb - 1, 0), 0)),
            pl.BlockSpec((1, tm, D), lambda b, i: (b, i, 0)),
            pl.BlockSpec((1, HALO, D), lambda b, i: (b, jnp.minimum((i + 1) * rb, n_halo_blocks - 1), 0)),
            pl.BlockSpec((1, tm, D), lambda b, i: (b, i, 0)),
            pl.BlockSpec((1, D), const),
            pl.BlockSpec((D, 4 * D), const),
            pl.BlockSpec((len(POOL_WINDOWS), D // 4, D // 4), lambda b, i: (0, 0, 0)),
            pl.BlockSpec((1, D), const),
            pl.BlockSpec((D, D), const),
            pl.BlockSpec((D, D), const),
            pl.BlockSpec((2, D), const),
            pl.BlockSpec((D, D), const),
            pl.BlockSpec((1, D), const),
        ],
        out_specs=pl.BlockSpec((1, tm, D), lambda b, i: (b, i, 0)),
        out_shape=jax.ShapeDtypeStruct((B, S, D), x.dtype),
        scratch_shapes=[pltpu.VMEM((tm + 2 * HALO, D), F32)],
        compiler_params=pltpu.CompilerParams(
            dimension_semantics=("arbitrary", "arbitrary"),
            vmem_limit_bytes=V7X_VMEM_LIMIT_BYTES),
        name="token_branch",
    )(x, x, x, yb, gpre, wt, mix, pscale, wpp, wpm, bm, wo, gpost)


def _layer(x, norm_pre_g, w_in, conv_qk_w, conv_qk_b, b_igate, b_fgate, pool_mix_w, pool_scale,
           mlstm_norm_g, w_proj_pool, w_proj_mlstm, b_merge, w_out, norm_post_g, *, tm):
    B, S, D = x.shape
    Dh = D // HEADS
    n_gate = 2 * N_DIRS * HEADS
    sizes = (D, D, D, D, D, D, D, n_gate, D, D)
    offs = [0]
    for s in sizes:
        offs.append(offs[-1] + s)
    col = lambda k: w_in[:, offs[k]:offs[k + 1]]
    w_ua, w_za, w_q, w_k, w_v, w_o, w_zb, w_g, w_ga, w_gb = (col(k) for k in range(10))

    gpre = norm_pre_g.reshape(1, D).astype(F32)
    per_head = lambda w: w.reshape(D, HEADS, Dh).transpose(1, 0, 2)
    wm = jnp.concatenate([per_head(w) for w in (w_q, w_k, w_v, w_o, w_zb)], axis=2).astype(BF16)
    wg4 = w_g.reshape(D, N_DIRS, 2, HEADS)
    wgt = jnp.concatenate([wg4[:, :, 0, :].reshape(D, N_DIRS * HEADS),
                           wg4[:, :, 1, :].reshape(D, N_DIRS * HEADS)], axis=1).T.astype(BF16)
    gbias = jnp.concatenate([b_igate.reshape(-1), b_fgate.reshape(-1)]).astype(F32)
    gbias = jnp.broadcast_to(gbias[:, None], (n_gate, CHUNK))
    cwb = jnp.concatenate([conv_qk_w, conv_qk_b[None, :],
                           jnp.zeros((8 - CONV_WIDTH - 1, 2 * D), F32)], axis=0)
    cw = jnp.concatenate([per_head_rows(cwb[:, :D], Dh), per_head_rows(cwb[:, D:], Dh)], axis=2)
    ng = mlstm_norm_g.reshape(HEADS, 1, Dh).astype(F32)

    yb = _mlstm_branch(x, gpre, wgt, gbias, wm, cw, ng)

    wt = jnp.concatenate([w_ua, w_za, w_ga, w_gb], axis=1).astype(BF16)
    return _token_branch(
        x, yb, gpre, wt, pool_mix_w.astype(BF16), pool_scale.reshape(1, D).astype(F32),
        w_proj_pool.astype(BF16), w_proj_mlstm.astype(BF16), b_merge.astype(F32),
        w_out.astype(BF16), norm_post_g.reshape(1, D).astype(F32), tm=tm)


def per_head_rows(w, dh):
    r = w.shape[0]
    return w.reshape(r, HEADS, dh).transpose(1, 0, 2)


def kernel(x, norm_pre_g, w_in, conv_qk_w, conv_qk_b, b_igate, b_fgate, pool_mix_w, pool_scale,
           mlstm_norm_g, w_proj_pool, w_proj_mlstm, b_merge, w_out, norm_post_g):
    depth = norm_pre_g.shape[0]
    tm = min(512, x.shape[1])
    for l in range(depth):
        x = _layer(x, norm_pre_g[l], w_in[l], conv_qk_w[l], conv_qk_b[l], b_igate[l], b_fgate[l],
                   pool_mix_w[l], pool_scale[l], mlstm_norm_g[l], w_proj_pool[l], w_proj_mlstm[l],
                   b_merge[l], w_out[l], norm_post_g[l], tm=tm)
    return x
```

```python
import functools

import jax
import jax.numpy as jnp
from jax import lax
from jax.experimental import pallas as pl
from jax.experimental.pallas import tpu as pltpu

F32 = jnp.float32
BF16 = jnp.bfloat16

HEADS = 4
N_DIRS = 2
POOL_WINDOWS = (2, 4, 8, 16)
CONV_WIDTH = 5
CHUNK = 128
EPS = 1e-6
NEG_INIT = -1e30
HALO = 8
LANES = 128
V7X_VMEM_LIMIT_BYTES = 56 * 1024 * 1024

G_A, G_EP, G_KEEP, G_MPREV, G_M, G_NM = 0, 2, 4, 6, 8, 10
G_ROWS = 16


def _sigmoid(x):
    return 1.0 / (1.0 + jnp.exp(-x))


def _silu(x):
    return x * _sigmoid(x)


def _log_sigmoid(x):
    return jnp.minimum(x, 0.0) - jnp.log1p(jnp.exp(-jnp.abs(x)))


def _rms_rows(x, g):
    ms = jnp.mean(x * x, axis=-1, keepdims=True)
    return x * lax.rsqrt(ms + EPS) * g


def _gate_tables(gt, gbias, gate_scr, *, nc):
    L = CHUNK
    lane = lax.broadcasted_iota(jnp.int32, (8, L), 1)
    row = lax.broadcasted_iota(jnp.int32, (8, L), 0)
    fwd = row < HEADS
    bi = gbias[0:8, :]
    bf = gbias[8:16, :]

    a_l, b_l, bl_l, mc_l, pm_l = [], [], [], [], []
    for c in range(nc):
        log_i = gt[0:8, c * L:(c + 1) * L] + bi
        log_f = _log_sigmoid(gt[8:16, c * L:(c + 1) * L] + bf)
        pre, suf = log_f, log_f
        sh = 1
        while sh < L:
            pre = pre + jnp.where(lane >= sh, pltpu.roll(pre, sh, 1), 0.0)
            suf = suf + jnp.where(lane < L - sh, pltpu.roll(suf, L - sh, 1), 0.0)
            sh *= 2
        b = jnp.where(fwd, pre, suf)
        bl = jnp.sum(log_f, axis=1, keepdims=True)
        a = log_i - b
        pmx, smx = a, a
        sh = 1
        while sh < L:
            pmx = jnp.maximum(pmx, jnp.where(lane >= sh, pltpu.roll(pmx, sh, 1), -jnp.inf))
            smx = jnp.maximum(smx, jnp.where(lane < L - sh, pltpu.roll(smx, L - sh, 1), -jnp.inf))
            sh *= 2
        a_l.append(a)
        b_l.append(b)
        bl_l.append(bl)
        mc_l.append(bl + jnp.max(a, axis=1, keepdims=True))
        pm_l.append(jnp.where(fwd, pmx, smx))

    fwd_col = fwd[:, 0:1]
    mprev = [None] * nc
    mnew = [None] * nc
    m_f = jnp.full((8, 1), NEG_INIT, F32)
    m_b = jnp.full((8, 1), NEG_INIT, F32)
    mprev_f, mnew_f, mprev_b, mnew_b = {}, {}, {}, {}
    for j in range(nc):
        cf, cb = j, nc - 1 - j
        mprev_f[cf] = m_f
        m_f = jnp.maximum(bl_l[cf] + m_f, mc_l[cf])
        mnew_f[cf] = m_f
        mprev_b[cb] = m_b
        m_b = jnp.maximum(bl_l[cb] + m_b, mc_l[cb])
        mnew_b[cb] = m_b
    for c in range(nc):
        mprev[c] = jnp.where(fwd_col, mprev_f[c], mprev_b[c])
        mnew[c] = jnp.where(fwd_col, mnew_f[c], mnew_b[c])

    for c in range(nc):
        a, b, bl = a_l[c], b_l[c], bl_l[c]
        mp = jnp.broadcast_to(mprev[c], (8, L))
        ep = jnp.exp(bl + a - mnew[c])
        keep = jnp.broadcast_to(jnp.exp(bl + mprev[c] - mnew[c]), (8, L))
        m_row = jnp.maximum(pm_l[c], mp)
        nm = jnp.exp(-(b + m_row))
        for hd in range(HEADS):
            for d in range(N_DIRS):
                r = d * HEADS + hd
                for base, val in ((G_A, a), (G_EP, ep), (G_KEEP, keep), (G_MPREV, mp),
                                  (G_M, m_row), (G_NM, nm)):
                    gate_scr[hd, c, base + d:base + d + 1, :] = val[r:r + 1, :]


def _mlstm_kernel(x_ref, gpre_ref, wgt_ref, gbias_ref, wm_ref, cw_ref, ng_ref,
                  yb_ref,
                  hn_scr, gate_scr, conv_scr, q_scr, k_scr, ktn_scr, v_scr, og_scr,
                  d_scr, cs_scr, nch_scr, c_scr,
                  col0_scr, col1_scr, pm0_scr, pm1_scr, qs0_scr, qs1_scr,
                  *, S, D, Dh, NC):
    L = CHUNK
    h = pl.program_id(1)

    @pl.when(h == 0)
    def _prep():
        gpre = gpre_ref[...]
        for blk in range(S // L):
            xb = x_ref[0, blk * L:(blk + 1) * L, :]
            hn_scr[blk * L:(blk + 1) * L, :] = _rms_rows(xb, gpre).astype(BF16)
        gt = lax.dot_general(wgt_ref[...], hn_scr[...], (((1,), (1,)), ((), ())),
                             preferred_element_type=F32)
        _gate_tables(gt, gbias_ref[...], gate_scr, nc=NC)
        zero_halo = jnp.zeros((HALO, LANES), F32)
        for t in range(2 * (Dh // LANES)):
            conv_scr[t, 0:HALO, :] = zero_halo
            conv_scr[t, HALO + S:HALO + S + HALO, :] = zero_halo

    NP = Dh // LANES
    hn = hn_scr[...]
    for t in range(2):
        pre = jnp.dot(hn, wm_ref[0, :, t * Dh:(t + 1) * Dh], preferred_element_type=F32)
        for p in range(NP):
            conv_scr[t * NP + p, HALO:HALO + S, :] = pre[:, p * LANES:(p + 1) * LANES]
    v_all = jnp.dot(hn, wm_ref[0, :, 2 * Dh:3 * Dh], preferred_element_type=F32)
    o_all = jnp.dot(hn, wm_ref[0, :, 3 * Dh:4 * Dh], preferred_element_type=F32)
    z_all = jnp.dot(hn, wm_ref[0, :, 4 * Dh:5 * Dh], preferred_element_type=F32)
    og_all = _sigmoid(o_all) * _silu(z_all)
    for c in range(NC):
        v_scr[c] = v_all[c * L:(c + 1) * L, :].astype(BF16)
        og_scr[c] = og_all[c * L:(c + 1) * L, :]

    qscale = Dh ** -0.5
    for c in range(NC):
        for t in range(2):
            planes = []
            for p in range(NP):
                cols = slice(t * Dh + p * LANES, t * Dh + (p + 1) * LANES)
                acc = jnp.broadcast_to(cw_ref[0, CONV_WIDTH:CONV_WIDTH + 1, cols], (L, LANES))
                for j in range(CONV_WIDTH):
                    off = HALO + c * L + j - CONV_WIDTH // 2
                    acc = acc + conv_scr[t * NP + p, off:off + L, :] * cw_ref[0, j:j + 1, cols]
                planes.append(acc)
            act = _silu(jnp.concatenate(planes, axis=1))
            if t == 0:
                q_scr[c] = (act * qscale).astype(BF16)
            else:
                k_scr[c] = act.astype(BF16)
                ktn_scr[c, :, 0:L] = act.T.astype(BF16)

    for d in range(N_DIRS):
        for c in range(NC):
            ep = gate_scr[h, c, G_EP + d:G_EP + d + 1, :]
            kw_t = ktn_scr[c, :, 0:L] * ep.astype(BF16)
            d_scr[c] = jnp.dot(kw_t, v_scr[c], preferred_element_type=F32)
            ep8 = jnp.broadcast_to(ep, (8, L)).astype(BF16)
            nch_scr[c] = jnp.dot(ep8, k_scr[c], preferred_element_type=F32)
        c_scr[...] = jnp.zeros((Dh, Dh), F32)
        n_st = jnp.zeros((8, Dh), F32)
        order = range(NC) if d == 0 else range(NC - 1, -1, -1)
        for c in order:
            cs_scr[d, c] = c_scr[...].astype(BF16)
            n_col = jnp.broadcast_to(n_st[0:1, :], (L, Dh)).T.astype(BF16)
            if d == 0:
                ktn_scr[c, :, L:2 * L] = n_col
            else:
                ktn_scr[c, :, L + d:L + d + 1] = n_col[:, d:d + 1]
            keep = gate_scr[h, c, G_KEEP + d:G_KEEP + d + 1, :]
            keep_d = jnp.concatenate([keep] * (Dh // L), axis=1)
            c_scr[...] = c_scr[...] * keep_d + d_scr[c]
            n_st = n_st * keep_d + nch_scr[c]

    li = lax.broadcasted_iota(jnp.int32, (L, L), 0)
    si = lax.broadcasted_iota(jnp.int32, (L, L), 1)
    masks = (si <= li, si >= li)
    ng = ng_ref[0]

    def stage_cols(c, col):
        for d in range(N_DIRS):
            for k, base in enumerate((G_M, G_NM)):
                row = gate_scr[h, c, base + d:base + d + 1, :]
                col[k * N_DIRS + d] = jnp.broadcast_to(row, (L, L)).T

    def stage_weights(c, col, pm_s, qs_s):
        q_c = q_scr[c]
        s_ext = jnp.dot(q_c, ktn_scr[c], preferred_element_type=F32)
        s_raw = s_ext[:, 0:L]
        qn = s_ext[:, L:2 * L]
        pm = jnp.zeros((L, L), F32)
        for d in range(N_DIRS):
            a_b = jnp.broadcast_to(gate_scr[h, c, G_A + d:G_A + d + 1, :], (L, L))
            mp_b = jnp.broadcast_to(gate_scr[h, c, G_MPREV + d:G_MPREV + d + 1, :], (L, L))
            m_c = col[d]
            p = jnp.where(masks[d], jnp.exp(a_b - m_c), 0.0) * s_raw
            inter = jnp.exp(mp_b - m_c)
            den = jnp.sum(p + inter * jnp.where(si == d, qn, 0.0), axis=1, keepdims=True)
            r = 1.0 / jnp.maximum(jnp.abs(den), col[N_DIRS + d])
            pm = pm + p * r
            s = (inter * r).astype(BF16)
            qs_s[d] = q_c * jnp.concatenate([s] * (Dh // L), axis=1)
        pm_s[...] = pm.astype(BF16)

    def stage_out(c, pm_s, qs_s):
        hb = jnp.dot(pm_s[...], v_scr[c], preferred_element_type=F32)
        for d in range(N_DIRS):
            hb = hb + jnp.dot(qs_s[d], cs_scr[d, c], preferred_element_type=F32)
        sq = hb * hb
        sq_l = sq[:, 0:LANES]
        for k in range(1, Dh // LANES):
            sq_l = sq_l + sq[:, k * LANES:(k + 1) * LANES]
        ms = jnp.sum(sq_l, axis=1, keepdims=True) * (1.0 / Dh)
        y = (hb * lax.rsqrt(ms + EPS) * ng) * og_scr[c]
        row0 = pl.multiple_of(c * L, L)
        yb_ref[0, pl.ds(row0, L), :] = y.astype(yb_ref.dtype)

    cols, pms, qss = (col0_scr, col1_scr), (pm0_scr, pm1_scr), (qs0_scr, qs1_scr)

    def pair(c, do_out, do_weights, do_cols):
        for k in range(2):
            if do_out:
                stage_out(c - 4 + k, pms[k], qss[k])
        for k in range(2):
            if do_weights:
                stage_weights(c - 2 + k, cols[k], pms[k], qss[k])
        for k in range(2):
            if do_cols:
                stage_cols(c + k, cols[k])

    pair(0, False, False, True)
    pair(2, False, True, True)

    def steady(j, carry):
        pair(2 * j + 4, True, True, True)
        return carry

    lax.fori_loop(0, (NC - 4) // 2, steady, 0)
    pair(NC, True, True, False)
    pair(NC + 2, True, False, False)


def _mlstm_branch(x, gpre, wgt, gbias, wm, cw, ng):
    B, S, D = x.shape
    Dh = D // HEADS
    NC = S // CHUNK
    L = CHUNK
    kern = functools.partial(_mlstm_kernel, S=S, D=D, Dh=Dh, NC=NC)
    return pl.pallas_call(
        kern,
        grid=(B, HEADS),
        in_specs=[
            pl.BlockSpec((1, S, D), lambda b, h: (b, 0, 0)),
            pl.BlockSpec((1, D), lambda b, h: (0, 0)),
            pl.BlockSpec((16, D), lambda b, h: (0, 0)),
            pl.BlockSpec((16, L), lambda b, h: (0, 0)),
            pl.BlockSpec((1, D, 5 * Dh), lambda b, h: (h, 0, 0)),
            pl.BlockSpec((1, 8, 2 * Dh), lambda b, h: (h, 0, 0)),
            pl.BlockSpec((1, 1, Dh), lambda b, h: (h, 0, 0)),
        ],
        out_specs=pl.BlockSpec((1, S, Dh), lambda b, h: (b, 0, h)),
        out_shape=jax.ShapeDtypeStruct((B, S, D), BF16),
        scratch_shapes=[
            pltpu.VMEM((S, D), BF16),
            pltpu.VMEM((HEADS, NC, G_ROWS, L), F32),
            pltpu.VMEM((2 * (Dh // LANES), S + 2 * HALO, LANES), F32),
            pltpu.VMEM((NC, L, Dh), BF16),
            pltpu.VMEM((NC, L, Dh), BF16),
            pltpu.VMEM((NC, Dh, 2 * L), BF16),
            pltpu.VMEM((NC, L, Dh), BF16),
            pltpu.VMEM((NC, L, Dh), F32),
            pltpu.VMEM((NC, Dh, Dh), F32),
            pltpu.VMEM((N_DIRS, NC, Dh, Dh), BF16),
            pltpu.VMEM((NC, 8, Dh), F32),
            pltpu.VMEM((Dh, Dh), F32),
            pltpu.VMEM((2 * N_DIRS, L, L), F32),
            pltpu.VMEM((2 * N_DIRS, L, L), F32),
            pltpu.VMEM((L, L), BF16),
            pltpu.VMEM((L, L), BF16),
            pltpu.VMEM((N_DIRS, L, Dh), BF16),
            pltpu.VMEM((N_DIRS, L, Dh), BF16),
        ],
        compiler_params=pltpu.CompilerParams(
            dimension_semantics=("arbitrary", "arbitrary"),
            vmem_limit_bytes=V7X_VMEM_LIMIT_BYTES),
        name="mlstm_branch",
    )(x, gpre, wgt, gbias, wm, cw, ng)


def _token_kernel(xp_ref, x_ref, xn_ref, yb_ref, gpre_ref, wt_ref, mix_ref, pscale_ref,
                  wpp_ref, wpm_ref, bm_ref, wo_ref, gpost_ref,
                  out_ref,
                  u_scr,
                  *, S, D, TM):
    i = pl.program_id(1)
    nt = pl.num_programs(1)
    Dg = D // len(POOL_WINDOWS)
    x = x_ref[0]
    x_ext = jnp.concatenate([xp_ref[0], x, xn_ref[0]], axis=0)
    hn_ext = _rms_rows(x_ext, gpre_ref[...]).astype(BF16)

    u_ext = jnp.dot(hn_ext, wt_ref[:, 0:D], preferred_element_type=F32)
    has_prev = (i > 0).astype(F32)
    has_next = (i < nt - 1).astype(F32)
    for p in range(D // LANES):
        cols = slice(p * LANES, (p + 1) * LANES)
        u_scr[p, 0:HALO, :] = u_ext[0:HALO, cols] * has_prev
        u_scr[p, HALO:HALO + TM, :] = u_ext[HALO:HALO + TM, cols]
        u_scr[p, HALO + TM:, :] = u_ext[HALO + TM:, cols] * has_next

    z_a = jnp.dot(hn_ext, wt_ref[:, D:2 * D], preferred_element_type=F32)[HALO:HALO + TM, :]
    g_a = jnp.dot(hn_ext, wt_ref[:, 2 * D:3 * D], preferred_element_type=F32)[HALO:HALO + TM, :]
    g_b = jnp.dot(hn_ext, wt_ref[:, 3 * D:4 * D], preferred_element_type=F32)[HALO:HALO + TM, :]

    t_glob = i * TM + lax.broadcasted_iota(jnp.int32, (TM, Dg), 0)
    ya_parts = []
    for g, w in enumerate(POOL_WINDOWS):
        cols = slice(g * Dg, (g + 1) * Dg)
        before, after = w // 2, w - 1 - w // 2
        wsums = []
        for p in range(g * Dg // LANES, (g + 1) * Dg // LANES):
            wsum = u_scr[p, HALO - before:HALO - before + TM, :]
            for j in range(-before + 1, after + 1):
                wsum = wsum + u_scr[p, HALO + j:HALO + j + TM, :]
            wsums.append(wsum)
        lo = jnp.maximum(t_glob - before, 0)
        hi = jnp.minimum(t_glob + after, S - 1)
        count = (hi - lo + 1).astype(F32)
        pooled = jnp.concatenate(wsums, axis=1) / count - u_ext[HALO:HALO + TM, cols]
        mixed = jnp.dot(pooled.astype(BF16), mix_ref[g], preferred_element_type=F32)
        ya_parts.append((mixed * pscale_ref[:, cols]) * _silu(z_a[:, cols]))
    y_a = jnp.concatenate(ya_parts, axis=1).astype(BF16)

    p_a = jnp.dot(y_a, wpp_ref[...], preferred_element_type=F32)
    p_b = jnp.dot(yb_ref[0], wpm_ref[...], preferred_element_type=F32)
    merged = _sigmoid(g_a + bm_ref[0:1, :]) * p_a + _sigmoid(g_b + bm_ref[1:2, :]) * p_b
    out = jnp.dot(merged.astype(BF16), wo_ref[...], preferred_element_type=F32)
    out_ref[0] = x + _rms_rows(out, gpost_ref[...])


def _token_branch(x, yb, gpre, wt, mix, pscale, wpp, wpm, bm, wo, gpost, *, tm):
    B, S, D = x.shape
    nt = S // tm
    rb = tm // HALO
    n_halo_blocks = S // HALO
    kern = functools.partial(_token_kernel, S=S, D=D, TM=tm)
    const = lambda b, i: (0, 0)
    return pl.pallas_call(
        kern,
        grid=(B, nt),
        in_specs=[
            pl.BlockSpec((1, HALO, D), lambda b, i: (b, jnp.maximum(i * rb - 1, 0), 0)),
            pl.BlockSpec((1, tm, D), lambda b, i: (b, i, 0)),
            pl.BlockSpec((1, HALO, D), lambda b, i: (b, jnp.minimum((i + 1) * rb, n_halo_blocks - 1), 0)),
            pl.BlockSpec((1, tm, D), lambda b, i: (b, i, 0)),
            pl.BlockSpec((1, D), const),
            pl.BlockSpec((D, 4 * D), const),
            pl.BlockSpec((len(POOL_WINDOWS), D // 4, D // 4), lambda b, i: (0, 0, 0)),
            pl.BlockSpec((1, D), const),
            pl.BlockSpec((D, D), const),
            pl.BlockSpec((D, D), const),
            pl.BlockSpec((2, D), const),
            pl.BlockSpec((D, D), const),
            pl.BlockSpec((1, D), const),
        ],
        out_specs=pl.BlockSpec((1, tm, D), lambda b, i: (b, i, 0)),
        out_shape=jax.ShapeDtypeStruct((B, S, D), x.dtype),
        scratch_shapes=[pltpu.VMEM((D // LANES, tm + 2 * HALO, LANES), F32)],
        compiler_params=pltpu.CompilerParams(
            dimension_semantics=("arbitrary", "arbitrary"),
            vmem_limit_bytes=V7X_VMEM_LIMIT_BYTES),
        name="token_branch",
    )(x, x, x, yb, gpre, wt, mix, pscale, wpp, wpm, bm, wo, gpost)


def _layer(x, norm_pre_g, w_in, conv_qk_w, conv_qk_b, b_igate, b_fgate, pool_mix_w, pool_scale,
           mlstm_norm_g, w_proj_pool, w_proj_mlstm, b_merge, w_out, norm_post_g, *, tm):
    B, S, D = x.shape
    Dh = D // HEADS
    n_gate = 2 * N_DIRS * HEADS
    sizes = (D, D, D, D, D, D, D, n_gate, D, D)
    offs = [0]
    for s in sizes:
        offs.append(offs[-1] + s)
    col = lambda k: w_in[:, offs[k]:offs[k + 1]]
    w_ua, w_za, w_q, w_k, w_v, w_o, w_zb, w_g, w_ga, w_gb = (col(k) for k in range(10))

    gpre = norm_pre_g.reshape(1, D).astype(F32)
    per_head = lambda w: w.reshape(D, HEADS, Dh).transpose(1, 0, 2)
    wm = jnp.concatenate([per_head(w) for w in (w_q, w_k, w_v, w_o, w_zb)], axis=2).astype(BF16)
    wg4 = w_g.reshape(D, N_DIRS, 2, HEADS)
    wgt = jnp.concatenate([wg4[:, :, 0, :].reshape(D, N_DIRS * HEADS),
                           wg4[:, :, 1, :].reshape(D, N_DIRS * HEADS)], axis=1).T.astype(BF16)
    gbias = jnp.concatenate([b_igate.reshape(-1), b_fgate.reshape(-1)]).astype(F32)
    gbias = jnp.broadcast_to(gbias[:, None], (n_gate, CHUNK))
    cwb = jnp.concatenate([conv_qk_w, conv_qk_b[None, :],
                           jnp.zeros((8 - CONV_WIDTH - 1, 2 * D), F32)], axis=0)
    cw = jnp.concatenate([per_head_rows(cwb[:, :D], Dh), per_head_rows(cwb[:, D:], Dh)], axis=2)
    ng = mlstm_norm_g.reshape(HEADS, 1, Dh).astype(F32)

    yb = _mlstm_branch(x, gpre, wgt, gbias, wm, cw, ng)

    wt = jnp.concatenate([w_ua, w_za, w_ga, w_gb], axis=1).astype(BF16)
    return _token_branch(
        x, yb, gpre, wt, pool_mix_w.astype(BF16), pool_scale.reshape(1, D).astype(F32),
        w_proj_pool.astype(BF16), w_proj_mlstm.astype(BF16), b_merge.astype(F32),
        w_out.astype(BF16), norm_post_g.reshape(1, D).astype(F32), tm=tm)


def per_head_rows(w, dh):
    r = w.shape[0]
    return w.reshape(r, HEADS, dh).transpose(1, 0, 2)


def kernel(x, norm_pre_g, w_in, conv_qk_w, conv_qk_b, b_igate, b_fgate, pool_mix_w, pool_scale,
           mlstm_norm_g, w_proj_pool, w_proj_mlstm, b_merge, w_out, norm_post_g):
    depth = norm_pre_g.shape[0]
    tm = min(512, x.shape[1])
    for l in range(depth):
        x = _layer(x, norm_pre_g[l], w_in[l], conv_qk_w[l], conv_qk_b[l], b_igate[l], b_fgate[l],
                   pool_mix_w[l], pool_scale[l], mlstm_norm_g[l], w_proj_pool[l], w_proj_mlstm[l],
                   b_merge[l], w_out[l], norm_post_g[l], tm=tm)
    return x
```
